```python
import jax, jax.numpy as jnp
from jax import lax
import numpy as np

D_MODEL = 1024
BATCH = 16
SEQ = 4096
DEPTH = 1
DEC_BATCH = 8
DEC_SEQ = 32
PAST_LEN = 1024

CHUNK = 64
PLE_DIM = 256
D_MIX = D_MODEL
CONV_CH = D_MIX // 2
CONV_WIDTH = 31
CONV_BUF = CONV_WIDTH - 1
HGRN_WIDTH = D_MIX - CONV_CH
HGRN_DK = 128
HGRN_DV = 128
HGRN_HEADS = HGRN_WIDTH // HGRN_DK
HGRN_BLOCK = 16
N_GROUPS = 4
EXPERTS_PER_GROUP = 4
TOP_K_INNER = 2
EXPERT_FF = D_MODEL // 4
SPLITS = [CONV_CH, 2 * CONV_CH, 2 * CONV_CH + HGRN_WIDTH, 2 * CONV_CH + 2 * HGRN_WIDTH, 2 * CONV_CH + 3 * HGRN_WIDTH]
D_IN = 2 * CONV_CH + 4 * HGRN_WIDTH
EPS = 1e-6

kernel_name = "hybrid_conv_hgrn2_hmoe_stream_step"


def _rmsnorm(x, w):
    xf = x.astype(jnp.float32)
    y = xf * lax.rsqrt(jnp.mean(xf * xf, axis=-1, keepdims=True) + EPS)
    return (y * w.astype(jnp.float32)).astype(x.dtype)


def _layernorm(x, w, b):
    xf = x.astype(jnp.float32)
    mu = jnp.mean(xf, axis=-1, keepdims=True)
    xc = xf - mu
    y = xc * lax.rsqrt(jnp.mean(xc * xc, axis=-1, keepdims=True) + EPS)
    return (y * w.astype(jnp.float32) + b.astype(jnp.float32)).astype(x.dtype)


def _conv_module(a, b, buf, w, bias, ln_w, ln_b):
    c = a * jax.nn.sigmoid(b)
    xc = jnp.concatenate([buf.astype(c.dtype), c], axis=1)
    y = lax.conv_general_dilated(
        xc, w[:, None, :].astype(c.dtype), window_strides=(1,), padding="VALID",
        dimension_numbers=("NWC", "WIO", "NWC"), feature_group_count=CONV_CH)
    y = _layernorm(y + bias.astype(y.dtype), ln_w, ln_b)
    return jax.nn.silu(y), xc[:, -CONV_BUF:].astype(buf.dtype)


def _hgrn2(q, logf, k, v, S0):
    B, T = q.shape[0], q.shape[1]
    pad = (-T) % HGRN_BLOCK

    def blocks(t):
        t = jnp.pad(t.astype(jnp.float32), ((0, 0), (0, pad), (0, 0), (0, 0)))
        n = t.shape[1] // HGRN_BLOCK
        return t.reshape(B, n, HGRN_BLOCK, t.shape[2], t.shape[3]).transpose(1, 0, 3, 2, 4)

    qb, gb, kb, vb = blocks(q), blocks(logf), blocks(k), blocks(v)
    causal = jnp.tril(jnp.ones((HGRN_BLOCK, HGRN_BLOCK), dtype=bool))

    def step(S, inp):
        qi, gi, ki, vi = inp
        bc = jnp.cumsum(gi, axis=2)
        inter = jnp.einsum("bhtk,bhkv->bhtv", qi * jnp.exp(bc), S)
        diff = bc[:, :, :, None, :] - bc[:, :, None, :, :]
        decay = jnp.exp(jnp.where(causal[:, :, None], diff, -jnp.inf))
        scores = jnp.einsum("bhtk,bhsk,bhtsk->bhts", qi, ki, decay)
        o = inter + jnp.einsum("bhts,bhsv->bhtv", scores, vi)
        bl = bc[:, :, -1:, :]
        S_new = jnp.exp(bl[:, :, 0, :])[..., None] * S + jnp.einsum("bhsk,bhsv->bhkv", ki * jnp.exp(bl - bc), vi)
        return S_new, o

    S_fin, ob = lax.scan(step, S0.astype(jnp.float32), (qb, gb, kb, vb))
    o = ob.transpose(1, 0, 3, 2, 4).reshape(B, -1, HGRN_HEADS, HGRN_DV)[:, :T]
    return o, S_fin


def _hmoe(u, w_group, b_group, w_router, b_router, w_e_gate, w_e_up, w_e_down):
    uf = u.astype(jnp.float32)
    grp_logits = uf @ w_group.astype(jnp.float32) + b_group.astype(jnp.float32)
    grp_prob = jax.nn.softmax(grp_logits, axis=-1)
    g_idx = jnp.argmax(grp_logits, axis=-1)
    g_onehot = jax.nn.one_hot(g_idx, N_GROUPS, dtype=jnp.float32)
    g_gate = jnp.sum(grp_prob * g_onehot, axis=-1, keepdims=True)
    exp_logits = jnp.einsum("nd,gde->nge", uf, w_router.astype(jnp.float32)) + b_router.astype(jnp.float32)
    chosen = jnp.einsum("nge,ng->ne", exp_logits, g_onehot)
    top_v, top_i = lax.top_k(chosen, TOP_K_INNER)
    top_w = jax.nn.softmax(top_v, axis=-1)
    within = jnp.sum(top_w[..., None] * jax.nn.one_hot(top_i, EXPERTS_PER_GROUP), axis=1)
    gates = ((g_gate * g_onehot)[:, :, None] * within[:, None, :]).astype(u.dtype)
    y = jnp.zeros_like(u)
    for g in range(N_GROUPS):
        hg = jax.nn.silu(jnp.einsum("nd,edf->nef", u, w_e_gate[g])) * jnp.einsum("nd,edf->nef", u, w_e_up[g])
        y = y + jnp.einsum("nef,efd->nd", hg * gates[:, g, :, None], w_e_down[g])
    return y


def _layer(h, p_l, conv_buf, S0, lb, g_mix, w_in, conv_w, conv_b, conv_ln_w, conv_ln_b, hgrn_norm_w,
           w_out, g_ffn, w_group, b_group, w_router, b_router, w_e_gate, w_e_up, w_e_down,
           g_ple, w_ple_gate, w_ple_proj):
    B, T, _ = h.shape
    u = _rmsnorm(h, g_mix)
    z = u @ w_in
    a, b, q, f_raw, i_in, g_out = jnp.split(z, SPLITS, axis=-1)
    conv_out, new_buf = _conv_module(a, b, conv_buf, conv_w, conv_b, conv_ln_w, conv_ln_b)
    fr = f_raw.astype(jnp.float32)
    lbf = lb.astype(jnp.float32)
    logf = jnp.logaddexp(jnp.log(lbf), jnp.log1p(-lbf) + jax.nn.log_sigmoid(fr))
    k = (1.0 - lbf) * jax.nn.sigmoid(-fr)
    heads = lambda t: t.reshape(B, T, HGRN_HEADS, HGRN_DK)
    o, S_new = _hgrn2(heads(q), heads(logf), heads(k), heads(i_in), S0)
    o = _rmsnorm(o, hgrn_norm_w) * jax.nn.silu(heads(g_out).astype(jnp.float32))
    hgrn_out = o.reshape(B, T, HGRN_WIDTH).astype(h.dtype)
    h = h + jnp.concatenate([conv_out, hgrn_out], axis=-1) @ w_out
    u2 = _rmsnorm(h, g_ffn)
    h = h + _hmoe(u2.reshape(B * T, D_MODEL), w_group, b_group, w_router, b_router,
                  w_e_gate, w_e_up, w_e_down).reshape(B, T, D_MODEL)
    gate = jax.nn.sigmoid(_rmsnorm(h, g_ple) @ w_ple_gate)
    h = h + gate * (p_l @ w_ple_proj)
    return h, new_buf, S_new.astype(S0.dtype)


def setup_inputs(seed: int = 0) -> dict:
    key = jax.random.key(seed)
    ks = jax.random.split(key, 32)
    f32 = jnp.float32
    nrm = lambda k, shape, s: jax.random.normal(k, shape, f32) * s
    G, E, F = N_GROUPS, EXPERTS_PER_GROUP, EXPERT_FF
    return {
        "x_prompt": nrm(ks[0], (BATCH, SEQ, D_MODEL), 1.0),
        "x_sample": nrm(ks[1], (DEC_BATCH, DEC_SEQ, D_MODEL), 1.0),
        "cache_conv": nrm(ks[2], (DEPTH, DEC_BATCH, CONV_BUF, CONV_CH), 0.5),
        "state_hgrn": nrm(ks[3], (DEPTH, DEC_BATCH, HGRN_HEADS, HGRN_DK, HGRN_DV), 0.5),
        "p_prompt": nrm(ks[4], (DEPTH, BATCH, SEQ, PLE_DIM), 1.0),
        "p_sample": nrm(ks[5], (DEPTH, DEC_BATCH, DEC_SEQ, PLE_DIM), 1.0),
        "g_mix": 1.0 + nrm(ks[6], (DEPTH, D_MODEL), 0.05),
        "w_in": nrm(ks[7], (DEPTH, D_MODEL, D_IN), D_MODEL ** -0.5),
        "conv_w": nrm(ks[8], (DEPTH, CONV_WIDTH, CONV_CH), CONV_WIDTH ** -0.5),
        "conv_b": nrm(ks[9], (DEPTH, CONV_CH), 0.02),
        "conv_ln_w": 1.0 + nrm(ks[10], (DEPTH, CONV_CH), 0.05),
        "conv_ln_b": nrm(ks[11], (DEPTH, CONV_CH), 0.02),
        "hgrn_lb": nrm(ks[12], (DEPTH + 1, HGRN_WIDTH), 0.1),
        "hgrn_norm_w": 1.0 + nrm(ks[13], (DEPTH, HGRN_DV), 0.05),
        "w_out": nrm(ks[14], (DEPTH, D_MIX, D_MODEL), D_MIX ** -0.5),
        "g_ffn": 1.0 + nrm(ks[15], (DEPTH, D_MODEL), 0.05),
        "w_group": nrm(ks[16], (DEPTH, D_MODEL, G), D_MODEL ** -0.5),
        "b_group": nrm(ks[17], (DEPTH, G), 0.01),
        "w_router": nrm(ks[18], (DEPTH, G, D_MODEL, E), D_MODEL ** -0.5),
        "b_router": nrm(ks[19], (DEPTH, G, E), 0.01),
        "w_e_gate": nrm(ks[20], (DEPTH, G, E, D_MODEL, F), D_MODEL ** -0.5),
        "w_e_up": nrm(ks[21], (DEPTH, G, E, D_MODEL, F), D_MODEL ** -0.5),
        "w_e_down": nrm(ks[22], (DEPTH, G, E, F, D_MODEL), F ** -0.5),
        "g_ple": 1.0 + nrm(ks[23], (DEPTH, D_MODEL), 0.05),
        "w_ple_gate": nrm(ks[24], (DEPTH, D_MODEL, D_MODEL), D_MODEL ** -0.5),
        "w_ple_proj": nrm(ks[25], (DEPTH, PLE_DIM, D_MODEL), PLE_DIM ** -0.5),
        "g_final": 1.0 + nrm(ks[26], (D_MODEL,), 0.05),
    }


def reference(x_prompt, x_sample, cache_conv, state_hgrn, p_prompt, p_sample, g_mix, w_in, conv_w, conv_b,
              conv_ln_w, conv_ln_b, hgrn_lb, hgrn_norm_w, w_out, g_ffn, w_group, b_group, w_router, b_router,
              w_e_gate, w_e_up, w_e_down, g_ple, w_ple_gate, w_ple_proj, g_final):
    lb_all = jnp.cumsum(jax.nn.softmax(hgrn_lb.astype(jnp.float32), axis=0), axis=0)
    hp, hs = x_prompt, x_sample
    bp_prompt = x_prompt.shape[0]
    conv_p, hgrn_p, conv_s, hgrn_s = [], [], [], []
    for l in range(DEPTH):
        wl = (g_mix[l], w_in[l], conv_w[l], conv_b[l], conv_ln_w[l], conv_ln_b[l], hgrn_norm_w[l], w_out[l],
              g_ffn[l], w_group[l], b_group[l], w_router[l], b_router[l], w_e_gate[l], w_e_up[l], w_e_down[l],
              g_ple[l], w_ple_gate[l], w_ple_proj[l])
        buf0 = jnp.zeros((bp_prompt, CONV_BUF, CONV_CH), cache_conv.dtype)
        S0 = jnp.zeros((bp_prompt, HGRN_HEADS, HGRN_DK, HGRN_DV), state_hgrn.dtype)
        hp, b_p, s_p = _layer(hp, p_prompt[l], buf0, S0, lb_all[l], *wl)
        hs, b_s, s_s = _layer(hs, p_sample[l], cache_conv[l], state_hgrn[l], lb_all[l], *wl)
        conv_p.append(b_p); hgrn_p.append(s_p); conv_s.append(b_s); hgrn_s.append(s_s)
    y_prompt = _rmsnorm(hp, g_final)
    y_sample = _rmsnorm(hs, g_final)
    return (y_prompt, y_sample, jnp.stack(conv_p), jnp.stack(hgrn_p), jnp.stack(conv_s), jnp.stack(hgrn_s))
```

```python
import functools

import jax
import jax.numpy as jnp
from jax import lax
from jax.experimental import pallas as pl
from jax.experimental.pallas import tpu as pltpu

EPS = 1e-6
CONV_WIDTH = 31
CONV_BUF = CONV_WIDTH - 1
HIST = 32
HEAD = 128
N_GROUPS = 4
N_EXPERTS = 4
EXP_CLAMP = 80.0
VMEM_LIMIT_BYTES = 56 * 1024 * 1024


def _bf16(x):
    return x.astype(jnp.bfloat16)


def _dot(a, b):
    return jnp.dot(a, b, preferred_element_type=jnp.float32)


def _dot_nt(a, b):
    return lax.dot_general(a, b, (((1,), (1,)), ((), ())), preferred_element_type=jnp.float32)


def _dot_tn(a, b):
    return lax.dot_general(a, b, (((0,), (0,)), ((), ())), preferred_element_type=jnp.float32)


def _split3(x):
    hi = _bf16(x)
    r1 = x - hi.astype(jnp.float32)
    mid = _bf16(r1)
    lo = _bf16(r1 - mid.astype(jnp.float32))
    return hi, mid, lo


def _rms(x, w):
    return x * lax.rsqrt(jnp.mean(x * x, axis=-1, keepdims=True) + EPS) * w


def _mixer_kernel(x_ref, buf0_ref, s0_ref, gmix_ref, win_ref, convw_ref, convb_ref, lnw_ref, lnb_ref, lb_ref,
                  hnw_ref, wout_ref,
                  h1_ref, convnew_ref, snew_ref,
                  ext_ref, st_ref, z_ref, mix_ref,
                  *, n_streams, tile_t, chunk, conv_rows):
    t = pl.program_id(1)
    n_t = pl.num_programs(1)
    c_ch = convw_ref.shape[1]
    n_heads = st_ref.shape[1]
    rows = n_streams * tile_t

    @pl.when(t == 0)
    def _load_state():
        for bi in range(n_streams):
            ext_ref[bi, 0:HIST - CONV_BUF, :] = jnp.zeros((HIST - CONV_BUF, c_ch), jnp.float32)
            ext_ref[bi, HIST - CONV_BUF:HIST, :] = buf0_ref[bi]
            for h in range(n_heads):
                st_ref[bi, h] = s0_ref[bi, h].T

    x = x_ref[...].reshape(rows, x_ref.shape[2])
    u = _bf16(_rms(x, gmix_ref[...]))

    ab = _dot(u, win_ref[:, 0:2 * c_ch])
    c = ab[:, 0:c_ch] * jax.nn.sigmoid(ab[:, c_ch:2 * c_ch])
    for bi in range(n_streams):
        ext_ref[bi, HIST:HIST + tile_t, :] = c[bi * tile_t:(bi + 1) * tile_t]
    z_ref[...] = _dot(u, win_ref[:, 2 * c_ch:])

    convb = convb_ref[...]
    lnw = lnw_ref[...]
    lnb = lnb_ref[...]
    for bi in range(n_streams):
        for r0 in range(0, tile_t, conv_rows):
            acc = jnp.zeros((conv_rows, c_ch), jnp.float32)
            for j in range(CONV_WIDTH):
                start = HIST - CONV_BUF + r0 + j
                acc = acc + ext_ref[bi, start:start + conv_rows, :] * convw_ref[j:j + 1, :]
            y = acc + convb
            mu = jnp.mean(y, axis=-1, keepdims=True)
            yc = y - mu
            yn = yc * lax.rsqrt(jnp.mean(yc * yc, axis=-1, keepdims=True) + EPS) * lnw + lnb
            mix_ref[bi * tile_t + r0:bi * tile_t + r0 + conv_rows, 0:c_ch] = _bf16(yn * jax.nn.sigmoid(yn))

    lb = lb_ref[...]
    one_m_lb = 1.0 - lb
    hnw = hnw_ref[...]
    w_hg = n_heads * HEAD
    ti = lax.broadcasted_iota(jnp.int32, (chunk, chunk), 0)
    si = lax.broadcasted_iota(jnp.int32, (chunk, chunk), 1)
    causal = ti >= si
    tri = _bf16(jnp.where(causal, 1.0, 0.0))
    mid_row = chunk // 2 - 1
    for bi in range(n_streams):
        for c0 in range(0, tile_t, chunk):
            r0 = bi * tile_t + c0
            q = z_ref[r0:r0 + chunk, 0:w_hg]
            fr = z_ref[r0:r0 + chunk, w_hg:2 * w_hg]
            e = jnp.exp(-jnp.abs(fr))
            r = 1.0 / (1.0 + e)
            er = e * r
            pos = fr >= 0.0
            sig_p = jnp.where(pos, r, er)
            sig_n = jnp.where(pos, er, r)
            logf = jnp.log(lb + one_m_lb * sig_p)
            kk = one_m_lb * sig_n
            hi, mid, lo = _split3(logf)
            bc = _dot(tri, hi) + _dot(tri, mid) + _dot(tri, lo)
            bl = bc[chunk - 1:chunk, :]
            bm = bc[mid_row:mid_row + 1, :]
            q_m = _bf16(q * jnp.exp(jnp.minimum(bc - bm, EXP_CLAMP)))
            k_m = _bf16(kk * jnp.exp(jnp.minimum(bm - bc, EXP_CLAMP)))
            q_s = _bf16(q * jnp.exp(bc))
            k_e = _bf16(kk * jnp.exp(bl - bc))
            d_end = jnp.exp(bl)
            for h in range(n_heads):
                cs = slice(h * HEAD, (h + 1) * HEAD)
                v = _bf16(z_ref[r0:r0 + chunk, 2 * w_hg + h * HEAD:2 * w_hg + (h + 1) * HEAD])
                st = st_ref[bi, h]
                sc = jnp.where(causal, _dot_nt(q_m[:, cs], k_m[:, cs]), 0.0)
                o = _dot(_bf16(sc), v) + _dot_nt(q_s[:, cs], _bf16(st))
                st_ref[bi, h] = st * d_end[:, cs] + _dot_tn(v, k_e[:, cs])
                g = z_ref[r0:r0 + chunk, 3 * w_hg + h * HEAD:3 * w_hg + (h + 1) * HEAD]
                on = _rms(o, hnw)
                mix_ref[r0:r0 + chunk, c_ch + h * HEAD:c_ch + (h + 1) * HEAD] = _bf16(on * (g * jax.nn.sigmoid(g)))

    h1 = x + _dot(mix_ref[...], wout_ref[...])
    h1_ref[...] = h1.reshape(h1_ref.shape)

    for bi in range(n_streams):
        ext_ref[bi, 0:HIST, :] = ext_ref[bi, tile_t:tile_t + HIST, :]

    @pl.when(t == n_t - 1)
    def _store_state():
        for bi in range(n_streams):
            convnew_ref[bi] = ext_ref[bi, HIST - CONV_BUF:HIST, :]
            for h in range(n_heads):
                snew_ref[bi, h] = st_ref[bi, h].T


def _mixer_tiles(batch, seq):
    if seq >= 512 and seq % 512 == 0:
        return 1, 512, 128, 32
    if seq % 128 == 0:
        return 1, 128, 128, 32
    assert seq % 32 == 0, seq
    return batch, seq, min(seq, 128), 32


def _mixer(x, buf0, s0, gmix, win, convw, convb, lnw, lnb, lb, hnw, wout):
    batch, seq, d = x.shape
    c_ch = convw.shape[1]
    n_heads = s0.shape[1]
    w_hg = n_heads * HEAD
    n_streams, tile_t, chunk, conv_rows = _mixer_tiles(batch, seq)
    assert tile_t % chunk == 0 and chunk % 2 == 0 and tile_t % conv_rows == 0 and tile_t >= HIST
    grid = (batch // n_streams, seq // tile_t)
    full = lambda a: pl.BlockSpec(a.shape, lambda b, t: (0,) * a.ndim)
    kern = functools.partial(_mixer_kernel, n_streams=n_streams, tile_t=tile_t, chunk=chunk, conv_rows=conv_rows)
    return pl.pallas_call(
        kern,
        grid=grid,
        in_specs=[
            pl.BlockSpec((n_streams, tile_t, d), lambda b, t: (b, t, 0)),
            pl.BlockSpec((n_streams, CONV_BUF, c_ch), lambda b, t: (b, 0, 0)),
            pl.BlockSpec((n_streams, n_heads, HEAD, HEAD), lambda b, t: (b, 0, 0, 0)),
            full(gmix), full(win), full(convw), full(convb), full(lnw), full(lnb), full(lb), full(hnw), full(wout),
        ],
        out_specs=[
            pl.BlockSpec((n_streams, tile_t, d), lambda b, t: (b, t, 0)),
            pl.BlockSpec((n_streams, CONV_BUF, c_ch), lambda b, t: (b, 0, 0)),
            pl.BlockSpec((n_streams, n_heads, HEAD, HEAD), lambda b, t: (b, 0, 0, 0)),
        ],
        out_shape=[
            jax.ShapeDtypeStruct((batch, seq, d), jnp.float32),
            jax.ShapeDtypeStruct((batch, CONV_BUF, c_ch), jnp.float32),
            jax.ShapeDtypeStruct((batch, n_heads, HEAD, HEAD), jnp.float32),
        ],
        scratch_shapes=[
            pltpu.VMEM((n_streams, HIST + tile_t, c_ch), jnp.float32),
            pltpu.VMEM((n_streams, n_heads, HEAD, HEAD), jnp.float32),
            pltpu.VMEM((n_streams * tile_t, 4 * w_hg), jnp.float32),
            pltpu.VMEM((n_streams * tile_t, c_ch + w_hg), jnp.bfloat16),
        ],
        compiler_params=pltpu.CompilerParams(
            dimension_semantics=("arbitrary", "arbitrary"), vmem_limit_bytes=VMEM_LIMIT_BYTES),
        name="mixer",
    )(x, buf0, s0, gmix, win, convw, convb, lnw, lnb, lb, hnw, wout)


def _route(logits):
    gl = [logits[:, j:j + 1] for j in range(N_GROUPS)]
    mx = functools.reduce(jnp.maximum, gl)
    denom = functools.reduce(jnp.add, [jnp.exp(v - mx) for v in gl])
    g_gate = 1.0 / denom
    is_g, taken = [], jnp.zeros_like(mx, dtype=jnp.bool_)
    for j in range(N_GROUPS):
        hit = jnp.logical_and(gl[j] == mx, jnp.logical_not(taken))
        is_g.append(hit)
        taken = jnp.logical_or(taken, hit)
    ce = []
    for e_i in range(N_EXPERTS):
        v = jnp.zeros_like(mx)
        for j in range(N_GROUPS):
            col = N_GROUPS + j * N_EXPERTS + e_i
            v = jnp.where(is_g[j], logits[:, col:col + 1], v)
        ce.append(v)
    v1 = functools.reduce(jnp.maximum, ce)
    is1, taken = [], jnp.zeros_like(mx, dtype=jnp.bool_)
    for e_i in range(N_EXPERTS):
        hit = jnp.logical_and(ce[e_i] == v1, jnp.logical_not(taken))
        is1.append(hit)
        taken = jnp.logical_or(taken, hit)
    rest = [jnp.where(is1[e_i], -jnp.inf, ce[e_i]) for e_i in range(N_EXPERTS)]
    v2 = functools.reduce(jnp.maximum, rest)
    is2, taken = [], jnp.zeros_like(mx, dtype=jnp.bool_)
    for e_i in range(N_EXPERTS):
        hit = jnp.logical_and(jnp.logical_and(rest[e_i] == v2, jnp.logical_not(is1[e_i])), jnp.logical_not(taken))
        is2.append(hit)
        taken = jnp.logical_or(taken, hit)
    e21 = jnp.exp(v2 - v1)
    w1 = 1.0 / (1.0 + e21)
    w2 = e21 * w1
    within = [jnp.where(is1[e_i], w1, 0.0) + jnp.where(is2[e_i], w2, 0.0) for e_i in range(N_EXPERTS)]
    return [[jnp.where(is_g[j], g_gate * within[e_i], 0.0) for e_i in range(N_EXPERTS)] for j in range(N_GROUPS)]


def _ffn_kernel(h1_ref, p_ref, gffn_ref, wr_ref, br_ref, wgu_ref, wd_ref, gple_ref, wpg_ref, wpp_ref, gfin_ref,
                y_ref,
                u2_ref, gates_ref, acc_ref, *, ff):
    g = pl.program_id(1)

    @pl.when(g == 0)
    def _router():
        u2 = _rms(h1_ref[...], gffn_ref[...])
        u2_ref[...] = _bf16(u2)
        u_hi = _bf16(u2)
        u_lo = _bf16(u2 - u_hi.astype(jnp.float32))
        wr = wr_ref[...]
        w_hi = _bf16(wr)
        w_lo = _bf16(wr - w_hi.astype(jnp.float32))
        logits = _dot(u_hi, w_hi) + _dot(u_hi, w_lo) + _dot(u_lo, w_hi) + br_ref[...]
        gates = _route(logits)
        lane = lax.broadcasted_iota(jnp.int32, logits.shape, 1)
        for j in range(N_GROUPS):
            tile = jnp.zeros(logits.shape, jnp.float32)
            for e_i in range(N_EXPERTS):
                tile = jnp.where(lane == e_i, gates[j][e_i], tile)
            gates_ref[j] = tile
        acc_ref[...] = jnp.zeros_like(acc_ref)

    gu = _dot(u2_ref[...], wgu_ref[0])
    gt = gates_ref[g]
    half = N_EXPERTS * ff
    parts = []
    for e_i in range(N_EXPERTS):
        ga = gu[:, e_i * ff:(e_i + 1) * ff]
        up = gu[:, half + e_i * ff:half + (e_i + 1) * ff]
        parts.append(_bf16((ga * jax.nn.sigmoid(ga)) * up * gt[:, e_i:e_i + 1]))
    hg = jnp.concatenate(parts, axis=-1)
    acc_ref[...] += _dot(hg, wd_ref[0])

    @pl.when(g == pl.num_programs(1) - 1)
    def _finish():
        h2 = h1_ref[...] + acc_ref[...]
        gate = jax.nn.sigmoid(_dot(_bf16(_rms(h2, gple_ref[...])), wpg_ref[...]))
        h3 = h2 + gate * _dot(_bf16(p_ref[...]), wpp_ref[...])
        y_ref[...] = _rms(h3, gfin_ref[...])


def _ffn_tile(n):
    for tm in (512, 256, 128, 64, 32, 16, 8):
        if n % tm == 0:
            return tm
    raise ValueError(n)


def _ffn(h1, p, gffn, wr, br, wgu, wd, gple, wpg, wpp, gfin):
    n, d = h1.shape
    ple = p.shape[1]
    ff = wd.shape[1] // N_EXPERTS
    tm = _ffn_tile(n)
    full = lambda a: pl.BlockSpec(a.shape, lambda i, g: (0,) * a.ndim)
    return pl.pallas_call(
        functools.partial(_ffn_kernel, ff=ff),
        grid=(n // tm, N_GROUPS),
        in_specs=[
            pl.BlockSpec((tm, d), lambda i, g: (i, 0)),
            pl.BlockSpec((tm, ple), lambda i, g: (i, 0)),
            full(gffn), full(wr), full(br),
            pl.BlockSpec((1,) + wgu.shape[1:], lambda i, g: (g, 0, 0)),
            pl.BlockSpec((1,) + wd.shape[1:], lambda i, g: (g, 0, 0)),
            full(gple), full(wpg), full(wpp), full(gfin),
        ],
        out_specs=pl.BlockSpec((tm, d), lambda i, g: (i, 0)),
        out_shape=jax.ShapeDtypeStruct((n, d), jnp.float32),
        scratch_shapes=[
            pltpu.VMEM((tm, d), jnp.bfloat16),
            pltpu.VMEM((N_GROUPS, tm, 128), jnp.float32),
            pltpu.VMEM((tm, d), jnp.float32),
        ],
        compiler_params=pltpu.CompilerParams(
            dimension_semantics=("arbitrary", "arbitrary"), vmem_limit_bytes=VMEM_LIMIT_BYTES),
        name="ffn",
    )(h1, p, gffn, wr, br, wgu, wd, gple, wpg, wpp, gfin)


def kernel(x_prompt, x_sample, cache_conv, state_hgrn, p_prompt, p_sample, g_mix, w_in, conv_w, conv_b, conv_ln_w, conv_ln_b, hgrn_lb, hgrn_norm_w, w_out, g_ffn, w_group, b_group, w_router, b_router, w_e_gate, w_e_up, w_e_down, g_ple, w_ple_gate, w_ple_proj, g_final):
    depth = w_in.shape[0]
    f32 = jnp.float32
    row = lambda a: a.reshape(1, -1).astype(f32)
    lb_all = jnp.cumsum(jax.nn.softmax(hgrn_lb.astype(f32), axis=0), axis=0)
    bp, bs = x_prompt.shape[0], x_sample.shape[0]
    n_heads, c_ch, d = state_hgrn.shape[2], cache_conv.shape[3], x_prompt.shape[2]
    hp, hs = x_prompt, x_sample
    conv_p, hgrn_p, conv_s, hgrn_s = [], [], [], []
    for l in range(depth):
        n_g, n_e, _, ff = w_e_gate[l].shape
        wr = jnp.concatenate([w_group[l], jnp.transpose(w_router[l], (1, 0, 2)).reshape(d, n_g * n_e)], axis=1)
        wr = jnp.pad(wr.astype(f32), ((0, 0), (0, 128 - wr.shape[1])))
        br = jnp.pad(jnp.concatenate([b_group[l], b_router[l].reshape(-1)]).astype(f32), (0, 128 - n_g - n_g * n_e))
        to_cols = lambda w: jnp.transpose(w, (0, 2, 1, 3)).reshape(n_g, d, n_e * ff)
        wgu = _bf16(jnp.concatenate([to_cols(w_e_gate[l]), to_cols(w_e_up[l])], axis=2))
        wd = _bf16(w_e_down[l].reshape(n_g, n_e * ff, d))
        mixer_w = (row(g_mix[l]), _bf16(w_in[l]), conv_w[l].astype(f32), row(conv_b[l]), row(conv_ln_w[l]),
                   row(conv_ln_b[l]), row(lb_all[l]), row(hgrn_norm_w[l]), _bf16(w_out[l]))
        ffn_w = (row(g_ffn[l]), wr, row(br), wgu, wd, row(g_ple[l]), _bf16(w_ple_gate[l]), _bf16(w_ple_proj[l]))
        last = l == depth - 1
        gfin = row(g_final) if last else None
        assert last, "the ffn kernel applies the final norm; deeper stacks need it made optional"

        buf0 = jnp.zeros((bp, CONV_BUF, c_ch), cache_conv.dtype)
        s0 = jnp.zeros((bp, n_heads, HEAD, HEAD), state_hgrn.dtype)
        h1, b_p, s_p = _mixer(hp, buf0, s0, *mixer_w)
        hp = _ffn(h1.reshape(-1, d), p_prompt[l].reshape(-1, p_prompt.shape[-1]), *ffn_w, gfin).reshape(hp.shape)
        h1, b_s, s_s = _mixer(hs, cache_conv[l], state_hgrn[l], *mixer_w)
        hs = _ffn(h1.reshape(-1, d), p_sample[l].reshape(-1, p_sample.shape[-1]), *ffn_w, gfin).reshape(hs.shape)
        conv_p.append(b_p); hgrn_p.append(s_p); conv_s.append(b_s); hgrn_s.append(s_s)
    return (hp, hs, jnp.stack(conv_p), jnp.stack(hgrn_p), jnp.stack(conv_s), jnp.stack(hgrn_s))
```

```python
import functools

import jax
import jax.numpy as jnp
from jax import lax
from jax.experimental import pallas as pl
from jax.experimental.pallas import tpu as pltpu

EPS = 1e-6
CONV_WIDTH = 31
CONV_BUF = CONV_WIDTH - 1
HIST = 32
HEAD = 128
N_GROUPS = 4
N_EXPERTS = 4
EXP_CLAMP = 80.0
VMEM_LIMIT_BYTES = 56 * 1024 * 1024


def _bf16(x):
    return x.astype(jnp.bfloat16)


def _dot(a, b):
    return jnp.dot(a, b, preferred_element_type=jnp.float32)


def _dot_nt(a, b):
    return lax.dot_general(a, b, (((1,), (1,)), ((), ())), preferred_element_type=jnp.float32)


def _dot_tn(a, b):
    return lax.dot_general(a, b, (((0,), (0,)), ((), ())), preferred_element_type=jnp.float32)


def _split3(x):
    hi = _bf16(x)
    r1 = x - hi.astype(jnp.float32)
    mid = _bf16(r1)
    lo = _bf16(r1 - mid.astype(jnp.float32))
    return hi, mid, lo


def _rms(x, w):
    return x * lax.rsqrt(jnp.mean(x * x, axis=-1, keepdims=True) + EPS) * w


def _mixer_kernel(x_ref, buf0_ref, s0_ref, gmix_ref, win_ref, convw_ref, convb_ref, lnw_ref, lnb_ref, lb_ref,
                  hnw_ref, wout_ref,
                  h1_ref, convnew_ref, snew_ref,
                  ext_ref, st_ref, z_ref, mix_ref,
                  *, n_streams, tile_t, chunk, conv_rows):
    t = pl.program_id(1)
    n_t = pl.num_programs(1)
    c_ch = convw_ref.shape[1]
    n_heads = st_ref.shape[1]
    rows = n_streams * tile_t

    @pl.when(t == 0)
    def _load_state():
        for bi in range(n_streams):
            ext_ref[bi, 0:HIST - CONV_BUF, :] = jnp.zeros((HIST - CONV_BUF, c_ch), jnp.float32)
            ext_ref[bi, HIST - CONV_BUF:HIST, :] = buf0_ref[bi]
            for h in range(n_heads):
                st_ref[bi, h] = s0_ref[bi, h].T

    x = x_ref[...].reshape(rows, x_ref.shape[2])
    u = _bf16(_rms(x, gmix_ref[...]))

    ab = _dot(u, win_ref[:, 0:2 * c_ch])
    c = ab[:, 0:c_ch] * jax.nn.sigmoid(ab[:, c_ch:2 * c_ch])
    for bi in range(n_streams):
        ext_ref[bi, HIST:HIST + tile_t, :] = c[bi * tile_t:(bi + 1) * tile_t]
    z_ref[...] = _dot(u, win_ref[:, 2 * c_ch:])

    convb = convb_ref[...]
    lnw = lnw_ref[...]
    lnb = lnb_ref[...]
    for bi in range(n_streams):
        for r0 in range(0, tile_t, conv_rows):
            acc = jnp.zeros((conv_rows, c_ch), jnp.float32)
            for j in range(CONV_WIDTH):
                start = HIST - CONV_BUF + r0 + j
                acc = acc + ext_ref[bi, start:start + conv_rows, :] * convw_ref[j:j + 1, :]
            y = acc + convb
            mu = jnp.mean(y, axis=-1, keepdims=True)
            yc = y - mu
            yn = yc * lax.rsqrt(jnp.mean(yc * yc, axis=-1, keepdims=True) + EPS) * lnw + lnb
            mix_ref[bi * tile_t + r0:bi * tile_t + r0 + conv_rows, 0:c_ch] = _bf16(yn * jax.nn.sigmoid(yn))

    lb = lb_ref[...]
    one_m_lb = 1.0 - lb
    hnw = hnw_ref[...]
    w_hg = n_heads * HEAD
    ti = lax.broadcasted_iota(jnp.int32, (chunk, chunk), 0)
    si = lax.broadcasted_iota(jnp.int32, (chunk, chunk), 1)
    causal = ti >= si
    tri = _bf16(jnp.where(causal, 1.0, 0.0))
    mid_row = chunk // 2 - 1
    for bi in range(n_streams):
        for c0 in range(0, tile_t, chunk):
            r0 = bi * tile_t + c0
            q = z_ref[r0:r0 + chunk, 0:w_hg]
            fr = z_ref[r0:r0 + chunk, w_hg:2 * w_hg]
            e = jnp.exp(-jnp.abs(fr))
            r = 1.0 / (1.0 + e)
            er = e * r
            pos = fr >= 0.0
            sig_p = jnp.where(pos, r, er)
            sig_n = jnp.where(pos, er, r)
            logf = jnp.log(lb + one_m_lb * sig_p)
            kk = one_m_lb * sig_n
            hi, mid, lo = _split3(logf)
            bc = _dot(tri, hi) + _dot(tri, mid) + _dot(tri, lo)
            bl = bc[chunk - 1:chunk, :]
            bm = bc[mid_row:mid_row + 1, :]
            q_m = _bf16(q * jnp.exp(jnp.minimum(bc - bm, EXP_CLAMP)))
            k_m = _bf16(kk * jnp.exp(jnp.minimum(bm - bc, EXP_CLAMP)))
            q_s = _bf16(q * jnp.exp(bc))
            k_e = _bf16(kk * jnp.exp(bl - bc))
            d_end = jnp.exp(bl)
            for h in range(n_heads):
                cs = slice(h * HEAD, (h + 1) * HEAD)
                v = _bf16(z_ref[r0:r0 + chunk, 2 * w_hg + h * HEAD:2 * w_hg + (h + 1) * HEAD])
                st = st_ref[bi, h]
                sc = jnp.where(causal, _dot_nt(q_m[:, cs], k_m[:, cs]), 0.0)
                o = _dot(_bf16(sc), v) + _dot_nt(q_s[:, cs], _bf16(st))
                st_ref[bi, h] = st * d_end[:, cs] + _dot_tn(v, k_e[:, cs])
                g = z_ref[r0:r0 + chunk, 3 * w_hg + h * HEAD:3 * w_hg + (h + 1) * HEAD]
                on = _rms(o, hnw)
                mix_ref[r0:r0 + chunk, c_ch + h * HEAD:c_ch + (h + 1) * HEAD] = _bf16(on * (g * jax.nn.sigmoid(g)))

    h1 = x + _dot(mix_ref[...], wout_ref[...])
    h1_ref[...] = h1.reshape(h1_ref.shape)

    for bi in range(n_streams):
        ext_ref[bi, 0:HIST, :] = ext_ref[bi, tile_t:tile_t + HIST, :]

    @pl.when(t == n_t - 1)
    def _store_state():
        for bi in range(n_streams):
            convnew_ref[bi] = ext_ref[bi, HIST - CONV_BUF:HIST, :]
            for h in range(n_heads):
                snew_ref[bi, h] = st_ref[bi, h].T


def _mixer_tiles(batch, seq):
    if seq >= 512 and seq % 512 == 0:
        return 1, 512, 128, 32
    if seq % 128 == 0:
        return 1, 128, 128, 32
    assert seq % 32 == 0, seq
    return batch, seq, min(seq, 128), 32


def _mixer(x, buf0, s0, gmix, win, convw, convb, lnw, lnb, lb, hnw, wout):
    batch, seq, d = x.shape
    c_ch = convw.shape[1]
    n_heads = s0.shape[1]
    w_hg = n_heads * HEAD
    n_streams, tile_t, chunk, conv_rows = _mixer_tiles(batch, seq)
    assert tile_t % chunk == 0 and chunk % 2 == 0 and tile_t % conv_rows == 0 and tile_t >= HIST
    grid = (batch // n_streams, seq // tile_t)
    full = lambda a: pl.BlockSpec(a.shape, lambda b, t: (0,) * a.ndim)
    kern = functools.partial(_mixer_kernel, n_streams=n_streams, tile_t=tile_t, chunk=chunk, conv_rows=conv_rows)
    return pl.pallas_call(
        kern,
        grid=grid,
        in_specs=[
            pl.BlockSpec((n_streams, tile_t, d), lambda b, t: (b, t, 0)),
            pl.BlockSpec((n_streams, CONV_BUF, c_ch), lambda b, t: (b, 0, 0)),
            pl.BlockSpec((n_streams, n_heads, HEAD, HEAD), lambda b, t: (b, 0, 0, 0)),
            full(gmix), full(win), full(convw), full(convb), full(lnw), full(lnb), full(lb), full(hnw), full(wout),
        ],
        out_specs=[
            pl.BlockSpec((n_streams, tile_t, d), lambda b, t: (b, t, 0)),
            pl.BlockSpec((n_streams, CONV_BUF, c_ch), lambda b, t: (b, 0, 0)),
            pl.BlockSpec((n_streams, n_heads, HEAD, HEAD), lambda b, t: (b, 0, 0, 0)),
        ],
        out_shape=[
            jax.ShapeDtypeStruct((batch, seq, d), jnp.float32),
            jax.ShapeDtypeStruct((batch, CONV_BUF, c_ch), jnp.float32),
            jax.ShapeDtypeStruct((batch, n_heads, HEAD, HEAD), jnp.float32),
        ],
        scratch_shapes=[
            pltpu.VMEM((n_streams, HIST + tile_t, c_ch), jnp.float32),
            pltpu.VMEM((n_streams, n_heads, HEAD, HEAD), jnp.float32),
            pltpu.VMEM((n_streams * tile_t, 4 * w_hg), jnp.float32),
            pltpu.VMEM((n_streams * tile_t, c_ch + w_hg), jnp.bfloat16),
        ],
        compiler_params=pltpu.CompilerParams(
            dimension_semantics=("arbitrary", "arbitrary"), vmem_limit_bytes=VMEM_LIMIT_BYTES),
        name="mixer",
    )(x, buf0, s0, gmix, win, convw, convb, lnw, lnb, lb, hnw, wout)


def _route(logit):
    gl = [logit(j) for j in range(N_GROUPS)]
    mx = functools.reduce(jnp.maximum, gl)
    denom = functools.reduce(jnp.add, [jnp.exp(v - mx) for v in gl])
    g_gate = 1.0 / denom
    is_g, taken = [], jnp.zeros_like(mx, dtype=jnp.bool_)
    for j in range(N_GROUPS):
        hit = jnp.logical_and(gl[j] == mx, jnp.logical_not(taken))
        is_g.append(hit)
        taken = jnp.logical_or(taken, hit)
    ce = []
    for e_i in range(N_EXPERTS):
        v = jnp.zeros_like(mx)
        for j in range(N_GROUPS):
            col = N_GROUPS + j * N_EXPERTS + e_i
            v = jnp.where(is_g[j], logit(col), v)
        ce.append(v)
    v1 = functools.reduce(jnp.maximum, ce)
    is1, taken = [], jnp.zeros_like(mx, dtype=jnp.bool_)
    for e_i in range(N_EXPERTS):
        hit = jnp.logical_and(ce[e_i] == v1, jnp.logical_not(taken))
        is1.append(hit)
        taken = jnp.logical_or(taken, hit)
    rest = [jnp.where(is1[e_i], -jnp.inf, ce[e_i]) for e_i in range(N_EXPERTS)]
    v2 = functools.reduce(jnp.maximum, rest)
    is2, taken = [], jnp.zeros_like(mx, dtype=jnp.bool_)
    for e_i in range(N_EXPERTS):
        hit = jnp.logical_and(jnp.logical_and(rest[e_i] == v2, jnp.logical_not(is1[e_i])), jnp.logical_not(taken))
        is2.append(hit)
        taken = jnp.logical_or(taken, hit)
    e21 = jnp.exp(v2 - v1)
    w1 = 1.0 / (1.0 + e21)
    w2 = e21 * w1
    gates = [g_gate * (jnp.where(is1[e_i], w1, 0.0) + jnp.where(is2[e_i], w2, 0.0)) for e_i in range(N_EXPERTS)]
    return is_g, gates


def _ffn_kernel(h1_ref, p_ref, gffn_ref, wrt_ref, brt_ref, wgu_ref, wd_ref, gple_ref, wpg_ref, wpp_ref, gfin_ref,
                y_ref,
                u2_ref, acc_ref, upper_ref, *, ff, sub):
    tm = h1_ref.shape[0]
    f32 = jnp.float32

    @pl.when(pl.program_id(0) == 0)
    def _init():
        ni = lax.broadcasted_iota(jnp.int32, (tm, tm), 0)
        nj = lax.broadcasted_iota(jnp.int32, (tm, tm), 1)
        upper_ref[...] = _bf16(jnp.where(ni < nj, 1.0, 0.0))

    h1 = h1_ref[...]
    u2 = _rms(h1, gffn_ref[...])
    u_hi = _bf16(u2)
    u_lo = _bf16(u2 - u_hi.astype(f32))
    u2_ref[...] = u_hi
    wrt = wrt_ref[...]
    w_hi = _bf16(wrt)
    w_lo = _bf16(wrt - w_hi.astype(f32))
    lt = _dot_nt(w_hi, u_hi) + _dot_nt(w_lo, u_hi) + _dot_nt(w_hi, u_lo) + brt_ref[...]
    is_g, gates = _route(lambda j: lt[j:j + 1, :])

    member = jnp.concatenate([jnp.where(m, 1.0, 0.0) for m in is_g] + [jnp.zeros((8 - N_GROUPS, tm), f32)], axis=0)
    rank = _dot(_bf16(member), upper_ref[...])
    g_hi = [_bf16(v) for v in gates]
    g_lo = [_bf16(v - h.astype(f32)) for v, h in zip(gates, g_hi)]
    gate_rows = jnp.concatenate(g_hi + g_lo, axis=0)

    acc_ref[...] = jnp.zeros_like(acc_ref)
    half = N_EXPERTS * ff
    slot = lax.broadcasted_iota(jnp.int32, (sub, tm), 0).astype(f32)
    for j in range(N_GROUPS):
        count = jnp.sum(jnp.where(is_g[j], 1, 0).astype(jnp.int32))
        rank_j = jnp.where(is_g[j], rank[j:j + 1, :], -1.0)

        def _sub_tile(k, carry, j=j, rank_j=rank_j):
            pick = _bf16(jnp.where(rank_j == slot + (k * sub).astype(f32), 1.0, 0.0))
            xg = _bf16(_dot(pick, u2_ref[...]))
            gs = _dot_nt(pick, gate_rows)
            gu = _dot(xg, wgu_ref[j])
            parts = []
            for e_i in range(N_EXPERTS):
                ga = gu[:, e_i * ff:(e_i + 1) * ff]
                up = gu[:, half + e_i * ff:half + (e_i + 1) * ff]
                ge = gs[:, e_i:e_i + 1] + gs[:, N_EXPERTS + e_i:N_EXPERTS + e_i + 1]
                parts.append(_bf16((ga * jax.nn.sigmoid(ga)) * up * ge))
            ys = _bf16(_dot(jnp.concatenate(parts, axis=-1), wd_ref[j]))
            acc_ref[...] += _dot_tn(pick, ys)
            return carry

        lax.fori_loop(0, (count + sub - 1) // sub, _sub_tile, 0)

    h2 = h1 + acc_ref[...]
    gate = jax.nn.sigmoid(_dot(_bf16(_rms(h2, gple_ref[...])), wpg_ref[...]))
    h3 = h2 + gate * _dot(_bf16(p_ref[...]), wpp_ref[...])
    y_ref[...] = _rms(h3, gfin_ref[...])


def _ffn_tiles(n):
    for tm in (512, 256, 128):
        if n % tm == 0:
            return tm, min(tm, 128)
    raise ValueError(n)


def _ffn(h1, p, gffn, wrt, brt, wgu, wd, gple, wpg, wpp, gfin):
    n, d = h1.shape
    ple = p.shape[1]
    ff = wd.shape[1] // N_EXPERTS
    tm, sub = _ffn_tiles(n)
    full = lambda a: pl.BlockSpec(a.shape, lambda i: (0,) * a.ndim, pipeline_mode=pl.Buffered(1))
    return pl.pallas_call(
        functools.partial(_ffn_kernel, ff=ff, sub=sub),
        grid=(n // tm,),
        in_specs=[
            pl.BlockSpec((tm, d), lambda i: (i, 0)),
            pl.BlockSpec((tm, ple), lambda i: (i, 0)),
            full(gffn), full(wrt), full(brt), full(wgu), full(wd), full(gple), full(wpg), full(wpp), full(gfin),
        ],
        out_specs=pl.BlockSpec((tm, d), lambda i: (i, 0)),
        out_shape=jax.ShapeDtypeStruct((n, d), jnp.float32),
        scratch_shapes=[
            pltpu.VMEM((tm, d), jnp.bfloat16),
            pltpu.VMEM((tm, d), jnp.float32),
            pltpu.VMEM((tm, tm), jnp.bfloat16),
        ],
        compiler_params=pltpu.CompilerParams(
            dimension_semantics=("arbitrary",), vmem_limit_bytes=VMEM_LIMIT_BYTES),
        name="ffn",
    )(h1, p, gffn, wrt, brt, wgu, wd, gple, wpg, wpp, gfin)


def kernel(x_prompt, x_sample, cache_conv, state_hgrn, p_prompt, p_sample, g_mix, w_in, conv_w, conv_b, conv_ln_w, conv_ln_b, hgrn_lb, hgrn_norm_w, w_out, g_ffn, w_group, b_group, w_router, b_router, w_e_gate, w_e_up, w_e_down, g_ple, w_ple_gate, w_ple_proj, g_final):
    depth = w_in.shape[0]
    f32 = jnp.float32
    row = lambda a: a.reshape(1, -1).astype(f32)
    lb_all = jnp.cumsum(jax.nn.softmax(hgrn_lb.astype(f32), axis=0), axis=0)
    bp, bs = x_prompt.shape[0], x_sample.shape[0]
    n_heads, c_ch, d = state_hgrn.shape[2], cache_conv.shape[3], x_prompt.shape[2]
    hp, hs = x_prompt, x_sample
    conv_p, hgrn_p, conv_s, hgrn_s = [], [], [], []
    for l in range(depth):
        n_g, n_e, _, ff = w_e_gate[l].shape
        wr = jnp.concatenate([w_group[l], jnp.transpose(w_router[l], (1, 0, 2)).reshape(d, n_g * n_e)], axis=1)
        wrt = jnp.pad(wr.astype(f32).T, ((0, 128 - wr.shape[1]), (0, 0)))
        br = jnp.pad(jnp.concatenate([b_group[l], b_router[l].reshape(-1)]).astype(f32), (0, 128 - n_g - n_g * n_e))
        to_cols = lambda w: jnp.transpose(w, (0, 2, 1, 3)).reshape(n_g, d, n_e * ff)
        wgu = _bf16(jnp.concatenate([to_cols(w_e_gate[l]), to_cols(w_e_up[l])], axis=2))
        wd = _bf16(w_e_down[l].reshape(n_g, n_e * ff, d))
        mixer_w = (row(g_mix[l]), _bf16(w_in[l]), conv_w[l].astype(f32), row(conv_b[l]), row(conv_ln_w[l]),
                   row(conv_ln_b[l]), row(lb_all[l]), row(hgrn_norm_w[l]), _bf16(w_out[l]))
        ffn_w = (row(g_ffn[l]), wrt, br.reshape(-1, 1), wgu, wd, row(g_ple[l]), _bf16(w_ple_gate[l]),
                 _bf16(w_ple_proj[l]))
        last = l == depth - 1
        gfin = row(g_final) if last else None
        assert last, "the ffn kernel applies the final norm; deeper stacks need it made optional"

        buf0 = jnp.zeros((bp, CONV_BUF, c_ch), cache_conv.dtype)
        s0 = jnp.zeros((bp, n_heads, HEAD, HEAD), state_hgrn.dtype)
        h1, b_p, s_p = _mixer(hp, buf0, s0, *mixer_w)
        hp = _ffn(h1.reshape(-1, d), p_prompt[l].reshape(-1, p_prompt.shape[-1]), *ffn_w, gfin).reshape(hp.shape)
        h1, b_s, s_s = _mixer(hs, cache_conv[l], state_hgrn[l], *mixer_w)
        hs = _ffn(h1.reshape(-1, d), p_sample[l].reshape(-1, p_sample.shape[-1]), *ffn_w, gfin).reshape(hs.shape)
        conv_p.append(b_p); hgrn_p.append(s_p); conv_s.append(b_s); hgrn_s.append(s_s)
    return (hp, hs, jnp.stack(conv_p), jnp.stack(hgrn_p), jnp.stack(conv_s), jnp.stack(hgrn_s))
```

```python
import functools

import jax
import jax.numpy as jnp
from jax import lax
from jax.experimental import pallas as pl
from jax.experimental.pallas import tpu as pltpu

EPS = 1e-6
SUBLANES = 8
LANES = 128
ROW_PAD = 2 * SUBLANES
CONV_WIDTH = 31
CONV_BUF = CONV_WIDTH - 1
HIST = 32
HEAD = 128
NORM_ROWS = 32
N_GROUPS = 4
N_EXPERTS = 4
EXP_CLAMP = 80.0
VMEM_LIMIT_BYTES = 56 * 1024 * 1024


def _bf16(x):
    return x.astype(jnp.bfloat16)


def _dot(a, b):
    return jnp.dot(a, b, preferred_element_type=jnp.float32)


def _dot_nt(a, b):
    return lax.dot_general(a, b, (((1,), (1,)), ((), ())), preferred_element_type=jnp.float32)


def _dot_tn(a, b):
    return lax.dot_general(a, b, (((0,), (0,)), ((), ())), preferred_element_type=jnp.float32)


def _split3(x):
    hi = _bf16(x)
    r1 = x - hi.astype(jnp.float32)
    mid = _bf16(r1)
    lo = _bf16(r1 - mid.astype(jnp.float32))
    return hi, mid, lo


def _rms(x, w):
    return x * lax.rsqrt(jnp.mean(x * x, axis=-1, keepdims=True) + EPS) * w


def _mixer_kernel(x_ref, buf0_ref, s0_ref, gmix_ref, win_ref, convw_ref, convb_ref, lnw_ref, lnb_ref, lb_ref,
                  hnw_ref, wout_ref,
                  h1_ref, convnew_ref, snew_ref,
                  ext_ref, st_ref, z_ref, y_ref, mix_ref,
                  *, n_streams, tile_t, chunk, conv_rows):
    t = pl.program_id(1)
    n_t = pl.num_programs(1)
    d = x_ref.shape[2]
    c_ch = convw_ref.shape[1]
    n_heads = st_ref.shape[1]
    w_hg = n_heads * HEAD
    rows = n_streams * tile_t

    @pl.when(t == 0)
    def _load_state():
        for bi in range(n_streams):
            ext_ref[bi, 0:HIST - CONV_BUF, :] = jnp.zeros((HIST - CONV_BUF, c_ch), jnp.float32)
            ext_ref[bi, HIST - CONV_BUF:HIST, :] = buf0_ref[bi]
            for h in range(n_heads):
                st_ref[bi, h] = s0_ref[bi, h].T

    x = x_ref[...].reshape(rows, d)
    u = _bf16(_rms(x, gmix_ref[...]))
    ab = _dot(u, win_ref[:, 0:2 * c_ch])
    c = ab[:, 0:c_ch] * jax.nn.sigmoid(ab[:, c_ch:2 * c_ch])
    for bi in range(n_streams):
        ext_ref[bi, HIST:HIST + tile_t, :] = c[bi * tile_t:(bi + 1) * tile_t]
    z_ref[...] = _dot(u, win_ref[:, 2 * c_ch:])

    first = HIST - CONV_BUF

    def conv(bi, t0):
        for l0 in range(0, c_ch, LANES):
            acc = None
            for p in range(SUBLANES):
                n_rows = conv_rows + (SUBLANES if p else 0)
                g_p = None
                for a in range(-(-(first + CONV_WIDTH) // SUBLANES)):
                    j = SUBLANES * a + p - first
                    if 0 <= j < CONV_WIDTH:
                        term = (ext_ref[bi, t0 + SUBLANES * a:t0 + SUBLANES * a + n_rows, l0:l0 + LANES]
                                * convw_ref[j:j + 1, l0:l0 + LANES])
                        g_p = term if g_p is None else g_p + term
                if g_p is not None:
                    shifted = g_p[p:p + conv_rows, :]
                    acc = shifted if acc is None else acc + shifted
            y_ref[bi * tile_t + t0:bi * tile_t + t0 + conv_rows, l0:l0 + LANES] = acc

    def norm_swish(r0):
        y = y_ref[r0:r0 + NORM_ROWS, :] + convb_ref[...]
        mu = jnp.mean(y, axis=-1, keepdims=True)
        yc = y - mu
        yn = yc * lax.rsqrt(jnp.mean(yc * yc, axis=-1, keepdims=True) + EPS) * lnw_ref[...] + lnb_ref[...]
        mix_ref[r0:r0 + NORM_ROWS, 0:c_ch] = _bf16(yn * jax.nn.sigmoid(yn))

    def recurrence(bi, c0):
        hnw = hnw_ref[...]
        ti = lax.broadcasted_iota(jnp.int32, (chunk, chunk), 0)
        si = lax.broadcasted_iota(jnp.int32, (chunk, chunk), 1)
        causal = ti >= si
        tri = _bf16(jnp.where(causal, 1.0, 0.0))
        r0 = bi * tile_t + c0
        lb = lb_ref[...]
        one_m_lb = 1.0 - lb
        q = z_ref[r0:r0 + chunk, 0:w_hg]
        fr = z_ref[r0:r0 + chunk, w_hg:2 * w_hg]
        e = jnp.exp(-jnp.abs(fr))
        r = 1.0 / (1.0 + e)
        er = e * r
        pos = fr >= 0.0
        sig_p = jnp.where(pos, r, er)
        sig_n = jnp.where(pos, er, r)
        logf = jnp.log(lb + one_m_lb * sig_p)
        kk = one_m_lb * sig_n
        hi, mid, lo = _split3(logf)
        bc = _dot(tri, hi) + _dot(tri, mid) + _dot(tri, lo)
        bl = bc[chunk - 1:chunk, :]
        bm = 0.5 * bl
        q_m = _bf16(q * jnp.exp(jnp.minimum(bc - bm, EXP_CLAMP)))
        k_m = _bf16(kk * jnp.exp(jnp.minimum(bm - bc, EXP_CLAMP)))
        q_s = _bf16(q * jnp.exp(bc))
        k_e = _bf16(kk * jnp.exp(bl - bc))
        d_end = jnp.exp(bl)
        for h in range(n_heads):
            cs = slice(h * HEAD, (h + 1) * HEAD)
            v = _bf16(z_ref[r0:r0 + chunk, 2 * w_hg + h * HEAD:2 * w_hg + (h + 1) * HEAD])
            st = st_ref[bi, h]
            sc = jnp.where(causal, _dot_nt(q_m[:, cs], k_m[:, cs]), 0.0)
            o = _dot(_bf16(sc), v) + _dot_nt(q_s[:, cs], _bf16(st))
            st_ref[bi, h] = st * d_end[:, cs] + _dot_tn(v, k_e[:, cs])
            g = z_ref[r0:r0 + chunk, 3 * w_hg + h * HEAD:3 * w_hg + (h + 1) * HEAD]
            on = _rms(o, hnw)
            mix_ref[r0:r0 + chunk, c_ch + h * HEAD:c_ch + (h + 1) * HEAD] = _bf16(on * (g * jax.nn.sigmoid(g)))

    for bi in range(n_streams):
        for t0 in range(0, tile_t, conv_rows):
            conv(bi, t0)
    for r0 in range(0, rows, NORM_ROWS):
        norm_swish(r0)
    for bi in range(n_streams):
        for c0 in range(0, tile_t, chunk):
            recurrence(bi, c0)
    h1 = x + _dot(mix_ref[...], wout_ref[...])
    h1_ref[...] = h1.reshape(h1_ref.shape)

    for bi in range(n_streams):
        ext_ref[bi, 0:HIST, :] = ext_ref[bi, tile_t:tile_t + HIST, :]

    @pl.when(t == n_t - 1)
    def _store_state():
        for bi in range(n_streams):
            convnew_ref[bi] = ext_ref[bi, HIST - CONV_BUF:HIST, :]
            for h in range(n_heads):
                snew_ref[bi, h] = st_ref[bi, h].T


def _mixer_tiles(batch, seq):
    chunk = 128
    if seq % 512 == 0:
        return 1, 512, chunk, 128
    if seq % chunk == 0:
        return 1, chunk, chunk, 64
    assert seq % HIST == 0, seq
    return batch, seq, min(seq, chunk), HIST


def _mixer(x, buf0, s0, gmix, win, convw, convb, lnw, lnb, lb, hnw, wout):
    batch, seq, d = x.shape
    c_ch = convw.shape[1]
    n_heads = s0.shape[1]
    w_hg = n_heads * HEAD
    n_streams, tile_t, chunk, conv_rows = _mixer_tiles(batch, seq)
    assert tile_t % chunk == 0 and tile_t % conv_rows == 0 and tile_t % NORM_ROWS == 0 and tile_t >= HIST
    grid = (batch // n_streams, seq // tile_t)
    full = lambda a: pl.BlockSpec(a.shape, lambda b, t: (0,) * a.ndim)
    kern = functools.partial(_mixer_kernel, n_streams=n_streams, tile_t=tile_t, chunk=chunk, conv_rows=conv_rows)
    return pl.pallas_call(
        kern,
        grid=grid,
        in_specs=[
            pl.BlockSpec((n_streams, tile_t, d), lambda b, t: (b, t, 0)),
            pl.BlockSpec((n_streams, CONV_BUF, c_ch), lambda b, t: (b, 0, 0)),
            pl.BlockSpec((n_streams, n_heads, HEAD, HEAD), lambda b, t: (b, 0, 0, 0)),
            full(gmix), full(win), full(convw), full(convb), full(lnw), full(lnb), full(lb), full(hnw), full(wout),
        ],
        out_specs=[
            pl.BlockSpec((n_streams, tile_t, d), lambda b, t: (b, t, 0)),
            pl.BlockSpec((n_streams, CONV_BUF, c_ch), lambda b, t: (b, 0, 0)),
            pl.BlockSpec((n_streams, n_heads, HEAD, HEAD), lambda b, t: (b, 0, 0, 0)),
        ],
        out_shape=[
            jax.ShapeDtypeStruct((batch, seq, d), jnp.float32),
            jax.ShapeDtypeStruct((batch, CONV_BUF, c_ch), jnp.float32),
            jax.ShapeDtypeStruct((batch, n_heads, HEAD, HEAD), jnp.float32),
        ],
        scratch_shapes=[
            pltpu.VMEM((n_streams, HIST + tile_t, c_ch), jnp.float32),
            pltpu.VMEM((n_streams, n_heads, HEAD, HEAD), jnp.float32),
            pltpu.VMEM((n_streams * tile_t, 4 * w_hg), jnp.float32),
            pltpu.VMEM((n_streams * tile_t, c_ch), jnp.float32),
            pltpu.VMEM((n_streams * tile_t, c_ch + w_hg), jnp.bfloat16),
        ],
        compiler_params=pltpu.CompilerParams(
            dimension_semantics=("arbitrary", "arbitrary"), vmem_limit_bytes=VMEM_LIMIT_BYTES),
        name="mixer",
    )(x, buf0, s0, gmix, win, convw, convb, lnw, lnb, lb, hnw, wout)


def _route(logit):
    gl = [logit(j) for j in range(N_GROUPS)]
    mx = functools.reduce(jnp.maximum, gl)
    denom = functools.reduce(jnp.add, [jnp.exp(v - mx) for v in gl])
    g_gate = 1.0 / denom
    is_g, taken = [], jnp.zeros_like(mx, dtype=jnp.bool_)
    for j in range(N_GROUPS):
        hit = jnp.logical_and(gl[j] == mx, jnp.logical_not(taken))
        is_g.append(hit)
        taken = jnp.logical_or(taken, hit)
    ce = []
    for e_i in range(N_EXPERTS):
        v = jnp.zeros_like(mx)
        for j in range(N_GROUPS):
            col = N_GROUPS + j * N_EXPERTS + e_i
            v = jnp.where(is_g[j], logit(col), v)
        ce.append(v)
    v1 = functools.reduce(jnp.maximum, ce)
    is1, taken = [], jnp.zeros_like(mx, dtype=jnp.bool_)
    for e_i in range(N_EXPERTS):
        hit = jnp.logical_and(ce[e_i] == v1, jnp.logical_not(taken))
        is1.append(hit)
        taken = jnp.logical_or(taken, hit)
    rest = [jnp.where(is1[e_i], -jnp.inf, ce[e_i]) for e_i in range(N_EXPERTS)]
    v2 = functools.reduce(jnp.maximum, rest)
    is2, taken = [], jnp.zeros_like(mx, dtype=jnp.bool_)
    for e_i in range(N_EXPERTS):
        hit = jnp.logical_and(jnp.logical_and(rest[e_i] == v2, jnp.logical_not(is1[e_i])), jnp.logical_not(taken))
        is2.append(hit)
        taken = jnp.logical_or(taken, hit)
    e21 = jnp.exp(v2 - v1)
    w1 = 1.0 / (1.0 + e21)
    w2 = e21 * w1
    gates = [g_gate * (jnp.where(is1[e_i], w1, 0.0) + jnp.where(is2[e_i], w2, 0.0)) for e_i in range(N_EXPERTS)]
    return is_g, gates


def _ffn_kernel(h1_ref, p_ref, gffn_ref, wrt_ref, brt_ref, wgu_ref, wd_ref, gple_ref, wpg_ref, wpp_ref, gfin_ref,
                y_ref,
                u2_ref, acc_ref, upper_ref, rank_ref, gate_ref, count_ref, *, ff, sub, last_sizes):
    tm = h1_ref.shape[0]
    n_logits = wrt_ref.shape[0]
    f32 = jnp.float32

    @pl.when(pl.program_id(0) == 0)
    def _init():
        ni = lax.broadcasted_iota(jnp.int32, (tm, tm), 0)
        nj = lax.broadcasted_iota(jnp.int32, (tm, tm), 1)
        upper_ref[...] = _bf16(jnp.where(ni < nj, 1.0, 0.0))

    h1 = h1_ref[...]
    u2 = _rms(h1, gffn_ref[...])
    u_hi = _bf16(u2)
    u_lo = _bf16(u2 - u_hi.astype(f32))
    u2_ref[...] = u_hi
    wrt = wrt_ref[...]
    w_hi = _bf16(wrt)
    w_lo = _bf16(wrt - w_hi.astype(f32))
    both = _dot_nt(jnp.concatenate([w_hi, w_lo], axis=0), u_hi)
    lt = both[0:n_logits] + both[n_logits:2 * n_logits] + _dot_nt(w_hi, u_lo) + brt_ref[...]
    is_g, gates = _route(lambda j: lt[j:j + 1, :])

    member = jnp.concatenate([jnp.where(m, 1.0, 0.0) for m in is_g]
                             + [jnp.zeros((SUBLANES - N_GROUPS, tm), f32)], axis=0)
    rank = _dot(_bf16(member), upper_ref[...])
    g_hi = [_bf16(v) for v in gates]
    g_lo = [_bf16(v - h.astype(f32)) for v, h in zip(gates, g_hi)]
    gate_ref[...] = jnp.concatenate(g_hi + g_lo, axis=0)
    for j in range(N_GROUPS):
        rank_ref[j:j + 1, :] = jnp.where(is_g[j], rank[j:j + 1, :], -1.0)
        count_ref[j] = jnp.sum(jnp.where(is_g[j], 1, 0).astype(jnp.int32))
    acc_ref[...] = jnp.zeros_like(acc_ref)
    half = N_EXPERTS * ff

    def _experts(j, first, n_rows):
        slot = lax.broadcasted_iota(jnp.int32, (n_rows, tm), 0).astype(f32) + first.astype(f32)
        pick = _bf16(jnp.where(rank_ref[pl.ds(j, 1), :] == slot, 1.0, 0.0))
        xg = _bf16(_dot(pick, u2_ref[...]))
        gs = _dot_nt(pick, gate_ref[...])
        gu = _dot(xg, wgu_ref[j])
        parts = []
        for e_i in range(N_EXPERTS):
            ga = gu[:, e_i * ff:(e_i + 1) * ff]
            up = gu[:, half + e_i * ff:half + (e_i + 1) * ff]
            ge = gs[:, e_i:e_i + 1] + gs[:, N_EXPERTS + e_i:N_EXPERTS + e_i + 1]
            parts.append(_bf16((ga * jax.nn.sigmoid(ga)) * up * ge))
        ys = _bf16(_dot(jnp.concatenate(parts, axis=-1), wd_ref[j]))
        acc_ref[...] += _dot_tn(pick, ys)

    def _group(j, carry):
        count = count_ref[j]
        n_bulk = jnp.maximum((count + sub - 1) // sub - 2, 0)

        def _bulk(k, c):
            _experts(j, k * sub, sub)
            return c

        lax.fori_loop(0, n_bulk, _bulk, 0)
        left = count - n_bulk * sub
        lo = 0
        for n_rows in last_sizes:
            @pl.when(jnp.logical_and(left > lo, left <= n_rows))
            def _last(n_rows=n_rows):
                _experts(j, n_bulk * sub, n_rows)
            lo = n_rows
        return carry

    lax.fori_loop(0, N_GROUPS, _group, 0)

    h2 = h1 + acc_ref[...]
    gate = jax.nn.sigmoid(_dot(_bf16(_rms(h2, gple_ref[...])), wpg_ref[...]))
    h3 = h2 + gate * _dot(_bf16(p_ref[...]), wpp_ref[...])
    y_ref[...] = _rms(h3, gfin_ref[...])


def _ffn_tiles(n):
    sub = 128
    for tm in (512, 256):
        if n % tm == 0:
            return tm, sub, (sub, sub + ROW_PAD, sub + 2 * ROW_PAD, sub + 4 * ROW_PAD, 2 * sub)
    raise ValueError(n)


def _ffn(h1, p, gffn, wrt, brt, wgu, wd, gple, wpg, wpp, gfin):
    n, d = h1.shape
    ple = p.shape[1]
    ff = wd.shape[1] // N_EXPERTS
    tm, sub, last_sizes = _ffn_tiles(n)
    full = lambda a: pl.BlockSpec(a.shape, lambda i: (0,) * a.ndim, pipeline_mode=pl.Buffered(1))
    return pl.pallas_call(
        functools.partial(_ffn_kernel, ff=ff, sub=sub, last_sizes=last_sizes),
        grid=(n // tm,),
        in_specs=[
            pl.BlockSpec((tm, d), lambda i: (i, 0)),
            pl.BlockSpec((tm, ple), lambda i: (i, 0)),
            full(gffn), full(wrt), full(brt), full(wgu), full(wd), full(gple), full(wpg), full(wpp), full(gfin),
        ],
        out_specs=pl.BlockSpec((tm, d), lambda i: (i, 0)),
        out_shape=jax.ShapeDtypeStruct((n, d), jnp.float32),
        scratch_shapes=[
            pltpu.VMEM((tm, d), jnp.bfloat16),
            pltpu.VMEM((tm, d), jnp.float32),
            pltpu.VMEM((tm, tm), jnp.bfloat16),
            pltpu.VMEM((SUBLANES, tm), jnp.float32),
            pltpu.VMEM((2 * N_EXPERTS, tm), jnp.bfloat16),
            pltpu.SMEM((N_GROUPS,), jnp.int32),
        ],
        compiler_params=pltpu.CompilerParams(
            dimension_semantics=("arbitrary",), vmem_limit_bytes=VMEM_LIMIT_BYTES),
        name="ffn",
    )(h1, p, gffn, wrt, brt, wgu, wd, gple, wpg, wpp, gfin)


def kernel(x_prompt, x_sample, cache_conv, state_hgrn, p_prompt, p_sample, g_mix, w_in, conv_w, conv_b, conv_ln_w, conv_ln_b, hgrn_lb, hgrn_norm_w, w_out, g_ffn, w_group, b_group, w_router, b_router, w_e_gate, w_e_up, w_e_down, g_ple, w_ple_gate, w_ple_proj, g_final):
    depth = w_in.shape[0]
    f32 = jnp.float32
    row = lambda a: a.reshape(1, -1).astype(f32)
    lb_all = jnp.cumsum(jax.nn.softmax(hgrn_lb.astype(f32), axis=0), axis=0)
    bp = x_prompt.shape[0]
    n_heads, c_ch, d = state_hgrn.shape[2], cache_conv.shape[3], x_prompt.shape[2]
    hp, hs = x_prompt, x_sample
    conv_p, hgrn_p, conv_s, hgrn_s = [], [], [], []
    for l in range(depth):
        n_g, n_e, _, ff = w_e_gate[l].shape
        wr = jnp.concatenate([w_group[l], jnp.transpose(w_router[l], (1, 0, 2)).reshape(d, n_g * n_e)], axis=1)
        n_logits = -(-wr.shape[1] // ROW_PAD) * ROW_PAD
        wrt = jnp.pad(wr.astype(f32).T, ((0, n_logits - wr.shape[1]), (0, 0)))
        br = jnp.pad(jnp.concatenate([b_group[l], b_router[l].reshape(-1)]).astype(f32),
                     (0, n_logits - n_g - n_g * n_e))
        to_cols = lambda w: jnp.transpose(w, (0, 2, 1, 3)).reshape(n_g, d, n_e * ff)
        wgu = _bf16(jnp.concatenate([to_cols(w_e_gate[l]), to_cols(w_e_up[l])], axis=2))
        wd = _bf16(w_e_down[l].reshape(n_g, n_e * ff, d))
        mixer_w = (row(g_mix[l]), _bf16(w_in[l]), conv_w[l].astype(f32), row(conv_b[l]), row(conv_ln_w[l]),
                   row(conv_ln_b[l]), row(lb_all[l]), row(hgrn_norm_w[l]), _bf16(w_out[l]))
        ffn_w = (row(g_ffn[l]), wrt, br.reshape(-1, 1), wgu, wd, row(g_ple[l]), _bf16(w_ple_gate[l]),
                 _bf16(w_ple_proj[l]))
        last = l == depth - 1
        gfin = row(g_final) if last else None
        assert last, "the ffn kernel applies the final norm; deeper stacks need it made optional"

        buf0 = jnp.zeros((bp, CONV_BUF, c_ch), cache_conv.dtype)
        s0 = jnp.zeros((bp, n_heads, HEAD, HEAD), state_hgrn.dtype)
        h1, b_p, s_p = _mixer(hp, buf0, s0, *mixer_w)
        hp = _ffn(h1.reshape(-1, d), p_prompt[l].reshape(-1, p_prompt.shape[-1]), *ffn_w, gfin).reshape(hp.shape)
        h1, b_s, s_s = _mixer(hs, cache_conv[l], state_hgrn[l], *mixer_w)
        hs = _ffn(h1.reshape(-1, d), p_sample[l].reshape(-1, p_sample.shape[-1]), *ffn_w, gfin).reshape(hs.shape)
        conv_p.append(b_p); hgrn_p.append(s_p); conv_s.append(b_s); hgrn_s.append(s_s)
    return (hp, hs, jnp.stack(conv_p), jnp.stack(hgrn_p), jnp.stack(conv_s), jnp.stack(hgrn_s))
```
